```python
import math
import jax, jax.numpy as jnp
from jax import lax
import numpy as np

D_MODEL = 1024
BATCH = 8
SEQ = 4096
DEPTH = 1

HEAD_DIM = 64
CONV_WIDTH = D_MODEL // 2
CONV_GROUPS = CONV_WIDTH // HEAD_DIM
CONV_K = 3
LRU_WIDTH = D_MODEL
LRU_HEADS = LRU_WIDTH // HEAD_DIM
LRU_CONV_K = 4
LRU_C = 8.0
MIX_WIDTH = CONV_WIDTH + LRU_WIDTH
IN_COLS = 3 * CONV_WIDTH + 2 * LRU_WIDTH
D_FF = 4 * D_MODEL
EPS = 1e-6

kernel_name = "hymba_style_shortconv_rglru_block"


def rmsnorm(x, g):
    xf = x.astype(jnp.float32)
    y = xf * lax.rsqrt(jnp.mean(xf * xf, axis=-1, keepdims=True) + EPS)
    return (y * g.astype(jnp.float32)).astype(x.dtype)


def causal_dwconv(x, w):
    k_len = w.shape[0]
    s = x.shape[1]
    xp = jnp.pad(x, ((0, 0), (k_len - 1, 0), (0, 0)))
    y = w[0] * xp[:, 0:s]
    for k in range(1, k_len):
        y = y + w[k] * xp[:, k:k + s]
    return y


def block_diag_linear(x, w, b):
    bt, s, _ = x.shape
    xh = x.reshape(bt, s, LRU_HEADS, HEAD_DIM)
    y = jnp.einsum('bshi,hij->bshj', xh, w).reshape(bt, s, LRU_WIDTH)
    return y + b


def rg_lru(x, w_a, b_a, w_x, b_x, lam):
    r = jax.nn.sigmoid(block_diag_linear(x, w_a, b_a).astype(jnp.float32))
    i = jax.nn.sigmoid(block_diag_linear(x, w_x, b_x).astype(jnp.float32))
    log_a = -LRU_C * r * jax.nn.softplus(-lam.astype(jnp.float32))
    a = jnp.exp(log_a)
    mult = jnp.sqrt(-jnp.expm1(2.0 * log_a))
    bx = mult * (i * x.astype(jnp.float32))

    def combine(lhs, rhs):
        a1, b1 = lhs
        a2, b2 = rhs
        return a1 * a2, a2 * b1 + b2

    _, h = lax.associative_scan(combine, (a, bx), axis=1)
    return h.astype(x.dtype)


def setup_inputs(seed: int = 0) -> dict:
    key = jax.random.key(seed)
    ks = jax.random.split(key, 20)
    L = DEPTH
    nrm = jax.random.normal
    x = nrm(ks[0], (BATCH, SEQ, D_MODEL), jnp.float32)
    norm_mix_g = 1.0 + 0.02 * nrm(ks[1], (L, D_MODEL), jnp.float32)
    w_in = nrm(ks[2], (L, D_MODEL, IN_COLS), jnp.float32) * D_MODEL ** -0.5
    conv_w = nrm(ks[3], (L, CONV_K, CONV_WIDTH), jnp.float32) * CONV_K ** -0.5
    rnn_conv_w = nrm(ks[4], (L, LRU_CONV_K, LRU_WIDTH), jnp.float32) * LRU_CONV_K ** -0.5
    rnn_conv_b = 0.01 * nrm(ks[5], (L, LRU_WIDTH), jnp.float32)
    w_a = nrm(ks[6], (L, LRU_HEADS, HEAD_DIM, HEAD_DIM), jnp.float32) * HEAD_DIM ** -0.5
    b_a = 0.01 * nrm(ks[7], (L, LRU_WIDTH), jnp.float32)
    w_x = nrm(ks[8], (L, LRU_HEADS, HEAD_DIM, HEAD_DIM), jnp.float32) * HEAD_DIM ** -0.5
    b_x = 0.01 * nrm(ks[9], (L, LRU_WIDTH), jnp.float32)
    a_c = jax.random.uniform(ks[10], (L, LRU_WIDTH), jnp.float32, 0.9, 0.999)
    s = a_c ** (1.0 / LRU_C)
    lru_lambda = jnp.log(s) - jnp.log1p(-s)
    g_norm_conv = 1.0 + 0.02 * nrm(ks[11], (L, CONV_WIDTH), jnp.float32)
    g_norm_rnn = 1.0 + 0.02 * nrm(ks[12], (L, LRU_WIDTH), jnp.float32)
    w_out = nrm(ks[13], (L, MIX_WIDTH, D_MODEL), jnp.float32) * MIX_WIDTH ** -0.5
    norm_mlp_g = 1.0 + 0.02 * nrm(ks[14], (L, D_MODEL), jnp.float32)
    w_mlp_in = nrm(ks[15], (L, D_MODEL, D_FF), jnp.float32) * D_MODEL ** -0.5
    w_mlp_out = nrm(ks[16], (L, D_FF, D_MODEL), jnp.float32) * D_FF ** -0.5
    final_norm_g = 1.0 + 0.02 * nrm(ks[17], (D_MODEL,), jnp.float32)
    return {"x": x, "norm_mix_g": norm_mix_g, "w_in": w_in, "conv_w": conv_w,
            "rnn_conv_w": rnn_conv_w, "rnn_conv_b": rnn_conv_b,
            "w_a": w_a, "b_a": b_a, "w_x": w_x, "b_x": b_x,
            "lru_lambda": lru_lambda, "g_norm_conv": g_norm_conv,
            "g_norm_rnn": g_norm_rnn, "w_out": w_out, "norm_mlp_g": norm_mlp_g,
            "w_mlp_in": w_mlp_in, "w_mlp_out": w_mlp_out,
            "final_norm_g": final_norm_g}


def reference(x, norm_mix_g, w_in, conv_w, rnn_conv_w, rnn_conv_b, w_a, b_a,
              w_x, b_x, lru_lambda, g_norm_conv, g_norm_rnn, w_out,
              norm_mlp_g, w_mlp_in, w_mlp_out, final_norm_g):
    split_pts = [CONV_WIDTH, 2 * CONV_WIDTH, 3 * CONV_WIDTH,
                 3 * CONV_WIDTH + LRU_WIDTH]
    for l in range(DEPTH):
        h = rmsnorm(x, norm_mix_g[l])
        u = jnp.einsum('bsd,dc->bsc', h, w_in[l])
        gate_b, gate_c, v, x_r, g = jnp.split(u, split_pts, axis=-1)
        y_conv = gate_b * causal_dwconv(gate_c * v, conv_w[l])
        xr = causal_dwconv(x_r, rnn_conv_w[l]) + rnn_conv_b[l]
        y_rnn = rg_lru(xr, w_a[l], b_a[l], w_x[l], b_x[l], lru_lambda[l])
        y_rnn = y_rnn * jax.nn.gelu(g)
        y = jnp.concatenate([rmsnorm(y_conv, g_norm_conv[l]),
                             rmsnorm(y_rnn, g_norm_rnn[l])], axis=-1)
        x = x + jnp.einsum('bsc,cd->bsd', y, w_out[l])
        h = rmsnorm(x, norm_mlp_g[l])
        z = jnp.square(jax.nn.relu(jnp.einsum('bsd,df->bsf', h, w_mlp_in[l])))
        x = x + jnp.einsum('bsf,fd->bsd', z, w_mlp_out[l])
    return rmsnorm(x, final_norm_g)
```

```python
import functools
import math

import jax
import jax.numpy as jnp
from jax import lax
from jax.experimental import pallas as pl
from jax.experimental.pallas import tpu as pltpu

HEAD_DIM = 64
CONV_K = 3
LRU_CONV_K = 4
LRU_C = 8.0
EPS = 1e-6

SUBLANES = 8
MXU_DIM = 256
HEADS_PER_GROUP = MXU_DIM // HEAD_DIM

MIX_TILE = 256
MLP_TILE = 512
FF_CHUNK = 1024
VMEM_LIMIT_BYTES = 56 * 1024 * 1024


def _rmsnorm(x, g):
    ms = jnp.mean(x * x, axis=-1, keepdims=True)
    return x * lax.rsqrt(ms + EPS) * g


def _softplus(x):
    return jnp.maximum(x, 0.0) + jnp.log1p(jnp.exp(-jnp.abs(x)))


def _gelu_tanh(x):
    c = math.sqrt(2.0 / math.pi)
    return 0.5 * x * (1.0 + jnp.tanh(c * (x + 0.044715 * (x * x * x))))


def _dot(a, b):
    return jnp.dot(a, b, preferred_element_type=jnp.float32)


def _causal_conv(buf_ref, val, w_ref, k_len, tile):
    buf_ref[pl.ds(SUBLANES, tile), :] = val
    y = w_ref[k_len - 1:k_len, :] * val
    for k in range(k_len - 1):
        shift = k_len - 1 - k
        y = y + w_ref[k:k + 1, :] * buf_ref[pl.ds(SUBLANES - shift, tile), :]
    buf_ref[pl.ds(0, SUBLANES), :] = buf_ref[pl.ds(tile, SUBLANES), :]
    return y


def _mixer_kernel(x_ref, g_mix_ref, w_in_ref, conv_w_ref, rconv_w_ref, rconv_b_ref, w_gate_ref,
                  b_a_ref, b_x_ref, lam_ref, g_conv_ref, g_rnn_ref, w_out_ref,
                  o_ref,
                  cv_buf, xr_buf, a_buf, b_buf, h_carry,
                  *, tile, conv_width, lru_width):
    cw, lw = conv_width, lru_width
    groups = tile // SUBLANES

    @pl.when(pl.program_id(1) == 0)
    def _():
        cv_buf[pl.ds(0, SUBLANES), :] = jnp.zeros((SUBLANES, cw), jnp.float32)
        xr_buf[pl.ds(0, SUBLANES), :] = jnp.zeros((SUBLANES, lw), jnp.float32)
        h_carry[...] = jnp.zeros((SUBLANES, lw), jnp.float32)

    x = x_ref[...]
    h = _rmsnorm(x, g_mix_ref[...]).astype(jnp.bfloat16)

    gate_b = _dot(h, w_in_ref[:, 0:cw])
    gate_c = _dot(h, w_in_ref[:, cw:2 * cw])
    v = _dot(h, w_in_ref[:, 2 * cw:3 * cw])
    y_conv = gate_b * _causal_conv(cv_buf, gate_c * v, conv_w_ref, CONV_K, tile)
    y_conv = _rmsnorm(y_conv, g_conv_ref[...]).astype(jnp.bfloat16)

    x_r = _dot(h, w_in_ref[:, 3 * cw:3 * cw + lw])
    xr = _causal_conv(xr_buf, x_r, rconv_w_ref, LRU_CONV_K, tile) + rconv_b_ref[...]
    xr_bf = xr.astype(jnp.bfloat16)
    r_parts, i_parts = [], []
    for grp in range(lw // MXU_DIM):
        ri = _dot(xr_bf[:, grp * MXU_DIM:(grp + 1) * MXU_DIM], w_gate_ref[grp])
        r_parts.append(ri[:, :MXU_DIM])
        i_parts.append(ri[:, MXU_DIM:])
    r = jax.nn.sigmoid(jnp.concatenate(r_parts, axis=-1) + b_a_ref[...])
    i = jax.nn.sigmoid(jnp.concatenate(i_parts, axis=-1) + b_x_ref[...])
    log_a = (-LRU_C * _softplus(-lam_ref[...])) * r
    a = jnp.exp(log_a)
    bx = jnp.sqrt(-jnp.tanh(log_a) * (a * a + 1.0)) * (i * xr)

    a3 = a.reshape(groups, SUBLANES, lw)
    b3 = bx.reshape(groups, SUBLANES, lw)
    row = lax.broadcasted_iota(jnp.int32, (groups, SUBLANES, lw), 1)
    for d in (1, 2, 4):
        keep = row >= d
        a_prev = jnp.where(keep, pltpu.roll(a3, d, axis=1), 1.0)
        b_prev = jnp.where(keep, pltpu.roll(b3, d, axis=1), 0.0)
        b3 = b3 + a3 * b_prev
        a3 = a3 * a_prev
    a_buf[...] = a3
    b_buf[...] = b3

    def group_step(j, carry):
        out = a_buf[j] * carry + b_buf[j]
        b_buf[j] = out
        return jnp.broadcast_to(out[SUBLANES - 1:SUBLANES, :], (SUBLANES, lw))

    h_carry[...] = lax.fori_loop(0, groups, group_step, h_carry[...], unroll=8)
    h_seq = b_buf[...].reshape(tile, lw)

    g = _dot(h, w_in_ref[:, 3 * cw + lw:3 * cw + 2 * lw])
    y_rnn = _rmsnorm(h_seq * _gelu_tanh(g), g_rnn_ref[...]).astype(jnp.bfloat16)

    o_ref[...] = x + _dot(y_conv, w_out_ref[0:cw, :]) + _dot(y_rnn, w_out_ref[cw:cw + lw, :])


def _mlp_kernel(x_ref, g_mlp_ref, w1_ref, w2_ref, g_final_ref, o_ref, *, d_ff):
    x = x_ref[...]
    h = _rmsnorm(x, g_mlp_ref[...]).astype(jnp.bfloat16)
    acc = x
    for c in range(d_ff // FF_CHUNK):
        z = jnp.maximum(_dot(h, w1_ref[:, c * FF_CHUNK:(c + 1) * FF_CHUNK]), 0.0)
        acc = acc + _dot((z * z).astype(jnp.bfloat16), w2_ref[c * FF_CHUNK:(c + 1) * FF_CHUNK, :])
    o_ref[...] = _rmsnorm(acc, g_final_ref[...])


def _const_spec(shape):
    nd = len(shape)
    return pl.BlockSpec(shape, lambda *_: (0,) * nd, pipeline_mode=pl.Buffered(1))


def _block_diag_groups(w):
    heads = w.shape[0]
    w4 = w.reshape(heads // HEADS_PER_GROUP, HEADS_PER_GROUP, HEAD_DIM, HEAD_DIM)
    eye = jnp.eye(HEADS_PER_GROUP, dtype=w.dtype)
    return jnp.einsum('gjik,jl->gjilk', w4, eye).reshape(heads // HEADS_PER_GROUP, MXU_DIM, MXU_DIM)


def _layer(x, norm_mix_g, w_in, conv_w, rnn_conv_w, rnn_conv_b, w_a, b_a, w_x, b_x, lru_lambda,
           g_norm_conv, g_norm_rnn, w_out, norm_mlp_g, w_mlp_in, w_mlp_out, out_norm_g):
    batch, seq, d_model = x.shape
    cw = conv_w.shape[-1]
    lw = rnn_conv_w.shape[-1]
    d_ff = w_mlp_in.shape[-1]
    assert seq % MIX_TILE == 0 and (batch * seq) % MLP_TILE == 0 and d_ff % FF_CHUNK == 0
    assert lw % MXU_DIM == 0 and w_in.shape[-1] == 3 * cw + 2 * lw

    bf = jnp.bfloat16
    row = lambda p: p.reshape(1, -1)
    w_gate = jnp.concatenate([_block_diag_groups(w_a), _block_diag_groups(w_x)], axis=-1).astype(bf)

    mixer_in = [row(norm_mix_g), w_in.astype(bf), conv_w, rnn_conv_w, row(rnn_conv_b), w_gate,
                row(b_a), row(b_x), row(lru_lambda), row(g_norm_conv), row(g_norm_rnn), w_out.astype(bf)]
    x_spec = pl.BlockSpec((None, MIX_TILE, d_model), lambda b, s: (b, s, 0))
    x1 = pl.pallas_call(
        functools.partial(_mixer_kernel, tile=MIX_TILE, conv_width=cw, lru_width=lw),
        grid=(batch, seq // MIX_TILE),
        in_specs=[x_spec] + [_const_spec(p.shape) for p in mixer_in],
        out_specs=x_spec,
        out_shape=jax.ShapeDtypeStruct(x.shape, jnp.float32),
        scratch_shapes=[
            pltpu.VMEM((MIX_TILE + SUBLANES, cw), jnp.float32),
            pltpu.VMEM((MIX_TILE + SUBLANES, lw), jnp.float32),
            pltpu.VMEM((MIX_TILE // SUBLANES, SUBLANES, lw), jnp.float32),
            pltpu.VMEM((MIX_TILE // SUBLANES, SUBLANES, lw), jnp.float32),
            pltpu.VMEM((SUBLANES, lw), jnp.float32),
        ],
        compiler_params=pltpu.CompilerParams(
            dimension_semantics=("arbitrary", "arbitrary"), vmem_limit_bytes=VMEM_LIMIT_BYTES),
        name="mixer",
    )(x, *mixer_in)

    tokens = batch * seq
    mlp_in = [row(norm_mlp_g), w_mlp_in.astype(bf), w_mlp_out.astype(bf), row(out_norm_g)]
    t_spec = pl.BlockSpec((MLP_TILE, d_model), lambda t: (t, 0))
    out = pl.pallas_call(
        functools.partial(_mlp_kernel, d_ff=d_ff),
        grid=(tokens // MLP_TILE,),
        in_specs=[t_spec] + [_const_spec(p.shape) for p in mlp_in],
        out_specs=t_spec,
        out_shape=jax.ShapeDtypeStruct((tokens, d_model), jnp.float32),
        compiler_params=pltpu.CompilerParams(
            dimension_semantics=("arbitrary",), vmem_limit_bytes=VMEM_LIMIT_BYTES),
        name="mlp",
    )(x1.reshape(tokens, d_model), *mlp_in)
    return out.reshape(batch, seq, d_model)


def kernel(x, norm_mix_g, w_in, conv_w, rnn_conv_w, rnn_conv_b, w_a, b_a, w_x, b_x, lru_lambda, g_norm_conv, g_norm_rnn, w_out, norm_mlp_g, w_mlp_in, w_mlp_out, final_norm_g):
    depth = w_in.shape[0]
    assert depth == 1, "the fused MLP + final-norm call assumes a single layer"
    return _layer(x, norm_mix_g[0], w_in[0], conv_w[0], rnn_conv_w[0], rnn_conv_b[0], w_a[0], b_a[0],
                  w_x[0], b_x[0], lru_lambda[0], g_norm_conv[0], g_norm_rnn[0], w_out[0],
                  norm_mlp_g[0], w_mlp_in[0], w_mlp_out[0], final_norm_g)
```

```python
import functools
import math

import jax
import jax.numpy as jnp
from jax import lax
from jax.experimental import pallas as pl
from jax.experimental.pallas import tpu as pltpu

HEAD_DIM = 64
CONV_K = 3
LRU_CONV_K = 4
LRU_C = 8.0
EPS = 1e-6

SUBLANES = 8
MXU_DIM = 256
HEADS_PER_GROUP = MXU_DIM // HEAD_DIM

MIX_STEPS = 64
MLP_TILE = 512
FF_CHUNK = 1024
VMEM_LIMIT_BYTES = 56 * 1024 * 1024


def _rmsnorm(x, g):
    ms = jnp.mean(x * x, axis=-1, keepdims=True)
    return x * lax.rsqrt(ms + EPS) * g


def _softplus(x):
    return jnp.maximum(x, 0.0) + jnp.log1p(jnp.exp(-jnp.abs(x)))


def _gelu_tanh(x):
    c = math.sqrt(2.0 / math.pi)
    return 0.5 * x * (1.0 + jnp.tanh(c * (x + 0.044715 * (x * x * x))))


def _dot(a, b):
    return jnp.dot(a, b, preferred_element_type=jnp.float32)


def _causal_conv(buf_ref, val, w_ref, k_len, rows, batch):
    head = (k_len - 1) * batch
    buf_ref[pl.ds(head, rows), :] = val
    y = w_ref[k_len - 1:k_len, :] * val
    for k in range(k_len - 1):
        y = y + w_ref[k:k + 1, :] * buf_ref[pl.ds(k * batch, rows), :]
    buf_ref[pl.ds(0, head), :] = buf_ref[pl.ds(rows, head), :]
    return y


def _tile_copies(hbm_ref, vmem_ref, sem, step, slot, steps, batch, to_vmem):
    copies = []
    for b in range(batch):
        hbm = hbm_ref.at[b, pl.ds(step * steps, steps), :]
        vmem = vmem_ref.at[slot, :, b, :]
        src, dst = (hbm, vmem) if to_vmem else (vmem, hbm)
        copies.append(pltpu.make_async_copy(src, dst, sem.at[slot]))
    return copies


def _mixer_kernel(x_hbm, g_mix_ref, w_in_ref, conv_w_ref, rconv_w_ref, rconv_b_ref, w_gate_ref,
                  b_a_ref, b_x_ref, lam_ref, g_conv_ref, g_rnn_ref, w_out_ref,
                  o_hbm,
                  x_buf, o_buf, in_sem, out_sem, cv_buf, xr_buf, h_carry,
                  *, steps, batch, conv_width, lru_width):
    cw, lw = conv_width, lru_width
    rows = steps * batch
    d_model = x_buf.shape[-1]
    s = pl.program_id(0)
    n = pl.num_programs(0)
    slot = lax.rem(s, 2)
    in_copies = functools.partial(_tile_copies, x_hbm, x_buf, in_sem, steps=steps, batch=batch, to_vmem=True)
    out_copies = functools.partial(_tile_copies, o_hbm, o_buf, out_sem, steps=steps, batch=batch, to_vmem=False)

    @pl.when(s == 0)
    def _():
        for c in in_copies(step=0, slot=0):
            c.start()
        cv_buf[pl.ds(0, (CONV_K - 1) * batch), :] = jnp.zeros(((CONV_K - 1) * batch, cw), jnp.float32)
        xr_buf[pl.ds(0, (LRU_CONV_K - 1) * batch), :] = jnp.zeros(((LRU_CONV_K - 1) * batch, lw), jnp.float32)
        h_carry[...] = jnp.zeros((batch, lw), jnp.float32)

    @pl.when(s + 1 < n)
    def _():
        for c in in_copies(step=s + 1, slot=1 - slot):
            c.start()

    for c in in_copies(step=s, slot=slot):
        c.wait()

    x = x_buf[slot].reshape(rows, d_model)
    h = _rmsnorm(x, g_mix_ref[...]).astype(jnp.bfloat16)

    gate_b = _dot(h, w_in_ref[:, 0:cw])
    gate_c = _dot(h, w_in_ref[:, cw:2 * cw])
    v = _dot(h, w_in_ref[:, 2 * cw:3 * cw])
    y_conv = gate_b * _causal_conv(cv_buf, gate_c * v, conv_w_ref, CONV_K, rows, batch)
    y_conv = _rmsnorm(y_conv, g_conv_ref[...]).astype(jnp.bfloat16)

    x_r = _dot(h, w_in_ref[:, 3 * cw:3 * cw + lw])
    xr = _causal_conv(xr_buf, x_r, rconv_w_ref, LRU_CONV_K, rows, batch) + rconv_b_ref[...]
    xr_bf = xr.astype(jnp.bfloat16)
    r_parts, i_parts = [], []
    for grp in range(lw // MXU_DIM):
        ri = _dot(xr_bf[:, grp * MXU_DIM:(grp + 1) * MXU_DIM], w_gate_ref[grp])
        r_parts.append(ri[:, :MXU_DIM])
        i_parts.append(ri[:, MXU_DIM:])
    r = jax.nn.sigmoid(jnp.concatenate(r_parts, axis=-1) + b_a_ref[...])
    i = jax.nn.sigmoid(jnp.concatenate(i_parts, axis=-1) + b_x_ref[...])
    log_a = (-LRU_C * _softplus(-lam_ref[...])) * r
    a = jnp.exp(log_a)
    bx = jnp.sqrt(-jnp.tanh(log_a) * (a * a + 1.0)) * (i * xr)

    h_t = h_carry[...]
    h_rows = []
    for j in range(steps):
        h_t = a[j * batch:(j + 1) * batch, :] * h_t + bx[j * batch:(j + 1) * batch, :]
        h_rows.append(h_t)
    h_carry[...] = h_t
    h_seq = jnp.concatenate(h_rows, axis=0)

    g = _dot(h, w_in_ref[:, 3 * cw + lw:3 * cw + 2 * lw])
    y_rnn = _rmsnorm(h_seq * _gelu_tanh(g), g_rnn_ref[...]).astype(jnp.bfloat16)

    y = x + _dot(y_conv, w_out_ref[0:cw, :]) + _dot(y_rnn, w_out_ref[cw:cw + lw, :])

    @pl.when(s >= 2)
    def _():
        for c in out_copies(step=s - 2, slot=slot):
            c.wait()

    o_buf[slot] = y.reshape(steps, batch, d_model)
    for c in out_copies(step=s, slot=slot):
        c.start()

    @pl.when(s == n - 1)
    def _():
        @pl.when(n > 1)
        def _():
            for c in out_copies(step=s - 1, slot=1 - slot):
                c.wait()
        for c in out_copies(step=s, slot=slot):
            c.wait()


def _mlp_kernel(x_ref, g_mlp_ref, w1_ref, w2_ref, g_final_ref, o_ref, *, d_ff):
    x = x_ref[...]
    h = _rmsnorm(x, g_mlp_ref[...]).astype(jnp.bfloat16)
    acc = x
    for c in range(d_ff // FF_CHUNK):
        z = jnp.maximum(_dot(h, w1_ref[:, c * FF_CHUNK:(c + 1) * FF_CHUNK]), 0.0)
        acc = acc + _dot((z * z).astype(jnp.bfloat16), w2_ref[c * FF_CHUNK:(c + 1) * FF_CHUNK, :])
    o_ref[...] = _rmsnorm(acc, g_final_ref[...])


def _const_spec(shape):
    nd = len(shape)
    return pl.BlockSpec(shape, lambda *_: (0,) * nd, pipeline_mode=pl.Buffered(1))


def _block_diag_groups(w):
    heads = w.shape[0]
    w4 = w.reshape(heads // HEADS_PER_GROUP, HEADS_PER_GROUP, HEAD_DIM, HEAD_DIM)
    eye = jnp.eye(HEADS_PER_GROUP, dtype=w.dtype)
    return jnp.einsum('gjik,jl->gjilk', w4, eye).reshape(heads // HEADS_PER_GROUP, MXU_DIM, MXU_DIM)


def _layer(x, norm_mix_g, w_in, conv_w, rnn_conv_w, rnn_conv_b, w_a, b_a, w_x, b_x, lru_lambda,
           g_norm_conv, g_norm_rnn, w_out, norm_mlp_g, w_mlp_in, w_mlp_out, out_norm_g):
    batch, seq, d_model = x.shape
    cw = conv_w.shape[-1]
    lw = rnn_conv_w.shape[-1]
    d_ff = w_mlp_in.shape[-1]
    assert batch == SUBLANES, "the mixer maps the batch onto the 8 sublanes of a vreg"
    assert seq % MIX_STEPS == 0 and (batch * seq) % MLP_TILE == 0 and d_ff % FF_CHUNK == 0
    assert lw % MXU_DIM == 0 and w_in.shape[-1] == 3 * cw + 2 * lw

    bf = jnp.bfloat16
    row = lambda p: p.reshape(1, -1)
    w_gate = jnp.concatenate([_block_diag_groups(w_a), _block_diag_groups(w_x)], axis=-1).astype(bf)

    mixer_in = [row(norm_mix_g), w_in.astype(bf), conv_w, rnn_conv_w, row(rnn_conv_b), w_gate,
                row(b_a), row(b_x), row(lru_lambda), row(g_norm_conv), row(g_norm_rnn), w_out.astype(bf)]
    rows = MIX_STEPS * batch
    x1 = pl.pallas_call(
        functools.partial(_mixer_kernel, steps=MIX_STEPS, batch=batch, conv_width=cw, lru_width=lw),
        grid=(seq // MIX_STEPS,),
        in_specs=[pl.BlockSpec(memory_space=pl.ANY)] + [_const_spec(p.shape) for p in mixer_in],
        out_specs=pl.BlockSpec(memory_space=pl.ANY),
        out_shape=jax.ShapeDtypeStruct(x.shape, jnp.float32),
        scratch_shapes=[
            pltpu.VMEM((2, MIX_STEPS, batch, d_model), jnp.float32),
            pltpu.VMEM((2, MIX_STEPS, batch, d_model), jnp.float32),
            pltpu.SemaphoreType.DMA((2,)),
            pltpu.SemaphoreType.DMA((2,)),
            pltpu.VMEM((rows + (CONV_K - 1) * batch, cw), jnp.float32),
            pltpu.VMEM((rows + (LRU_CONV_K - 1) * batch, lw), jnp.float32),
            pltpu.VMEM((batch, lw), jnp.float32),
        ],
        compiler_params=pltpu.CompilerParams(
            dimension_semantics=("arbitrary",), vmem_limit_bytes=VMEM_LIMIT_BYTES),
        name="mixer",
    )(x, *mixer_in)

    tokens = batch * seq
    mlp_in = [row(norm_mlp_g), w_mlp_in.astype(bf), w_mlp_out.astype(bf), row(out_norm_g)]
    t_spec = pl.BlockSpec((MLP_TILE, d_model), lambda t: (t, 0))
    out = pl.pallas_call(
        functools.partial(_mlp_kernel, d_ff=d_ff),
        grid=(tokens // MLP_TILE,),
        in_specs=[t_spec] + [_const_spec(p.shape) for p in mlp_in],
        out_specs=t_spec,
        out_shape=jax.ShapeDtypeStruct((tokens, d_model), jnp.float32),
        compiler_params=pltpu.CompilerParams(
            dimension_semantics=("arbitrary",), vmem_limit_bytes=VMEM_LIMIT_BYTES),
        name="mlp",
    )(x1.reshape(tokens, d_model), *mlp_in)
    return out.reshape(batch, seq, d_model)


def kernel(x, norm_mix_g, w_in, conv_w, rnn_conv_w, rnn_conv_b, w_a, b_a, w_x, b_x, lru_lambda, g_norm_conv, g_norm_rnn, w_out, norm_mlp_g, w_mlp_in, w_mlp_out, final_norm_g):
    depth = w_in.shape[0]
    assert depth == 1, "the fused MLP + final-norm call assumes a single layer"
    return _layer(x, norm_mix_g[0], w_in[0], conv_w[0], rnn_conv_w[0], rnn_conv_b[0], w_a[0], b_a[0],
                  w_x[0], b_x[0], lru_lambda[0], g_norm_conv[0], g_norm_rnn[0], w_out[0],
                  norm_mlp_g[0], w_mlp_in[0], w_mlp_out[0], final_norm_g)
```
